```python
import jax, jax.numpy as jnp
from jax import lax
import numpy as np

D_MODEL = 2048
BATCH = 2
SEQ = 16384
DEPTH = 2

D_MIX = D_MODEL
FOX_HEADS = 8
FOX_HEAD_DIM = 128
FOX_WIDTH = FOX_HEADS * FOX_HEAD_DIM
FOX_BLOCK = 128
RET_HEADS = 4
RET_HEAD_DIM = 128
RET_WIDTH = RET_HEADS * RET_HEAD_DIM
RET_CHUNK = 128
ROPE_BASE = 10000.0
RWKV_HEADS = 8
RWKV_HEAD_DIM = 64
RWKV_WIDTH = RWKV_HEADS * RWKV_HEAD_DIM
RWKV_DECAY_LORA = 96
RWKV_AAA_LORA = 96
RWKV_MV_LORA = 64
RWKV_GATE_LORA = 256
D_FF = 5632
NORM_EPS = 1e-6
RWKV_LNX_EPS = 64e-5

FOX_COLS = 4 * FOX_WIDTH + FOX_HEADS
RET_COLS = 4 * RET_WIDTH
RWKV_SHIFT_COLS = 3 * RWKV_WIDTH + RWKV_DECAY_LORA + RWKV_AAA_LORA + RWKV_GATE_LORA
N_IN_FIRST = FOX_COLS + RET_COLS + RWKV_SHIFT_COLS
N_IN_REST = N_IN_FIRST + RWKV_MV_LORA

kernel_name = 'hybrid_fox_retnet_rwkv7_macaron_sandwich'


def _split(z, sizes):
    idx = np.cumsum(sizes)[:-1].tolist()
    return jnp.split(z, idx, axis=-1)


def rms_norm(x, g, eps=NORM_EPS):
    xf = x.astype(jnp.float32)
    y = xf * lax.rsqrt(jnp.mean(xf * xf, axis=-1, keepdims=True) + eps)
    return (y * g.astype(jnp.float32)).astype(x.dtype)


def swiglu(u, w_gu, w_down):
    gate, up = jnp.split(u @ w_gu, 2, axis=-1)
    return (jax.nn.silu(gate) * up) @ w_down


def rotary(x, pos):
    half = x.shape[-1] // 2
    inv = ROPE_BASE ** (-jnp.arange(half, dtype=jnp.float32) / half)
    ang = pos[:, None] * inv[None, :]
    cos = jnp.cos(ang)[None, :, None, :]
    sin = jnp.sin(ang)[None, :, None, :]
    x1, x2 = x[..., :half], x[..., half:]
    return jnp.concatenate([x1 * cos - x2 * sin, x1 * sin + x2 * cos], axis=-1)


def forgetting_attention(p, qk_gain, f_bias):
    B, S, _ = p.shape
    f32 = jnp.float32
    q, k, v, og, f = _split(p, [FOX_WIDTH] * 4 + [FOX_HEADS])
    heads = lambda t: t.reshape(B, S, FOX_HEADS, FOX_HEAD_DIM).astype(f32).transpose(0, 2, 1, 3)
    q = rms_norm(heads(q), qk_gain[0]) * (FOX_HEAD_DIM ** -0.5)
    k = rms_norm(heads(k), qk_gain[1])
    v = heads(v)
    log_f = jax.nn.log_sigmoid(f.astype(f32) + f_bias.astype(f32))
    cum = jnp.cumsum(log_f, axis=1).transpose(0, 2, 1)
    nb = S // FOX_BLOCK
    q_blocks = q.reshape(B, FOX_HEADS, nb, FOX_BLOCK, FOX_HEAD_DIM).transpose(2, 0, 1, 3, 4)
    c_blocks = cum.reshape(B, FOX_HEADS, nb, FOX_BLOCK).transpose(2, 0, 1, 3)
    key_pos = jnp.arange(S)

    def attend(args):
        qb, cb, start = args
        logits = jnp.einsum('bhqd,bhkd->bhqk', qb, k) + cb[..., None] - cum[:, :, None, :]
        q_pos = start + jnp.arange(FOX_BLOCK)
        logits = jnp.where(key_pos[None, :] <= q_pos[:, None], logits, -jnp.inf)
        probs = jax.nn.softmax(logits, axis=-1)
        return jnp.einsum('bhqk,bhkd->bhqd', probs, v)

    o = lax.map(attend, (q_blocks, c_blocks, jnp.arange(nb) * FOX_BLOCK))
    o = o.transpose(1, 0, 3, 2, 4).reshape(B, S, FOX_WIDTH)
    return (o * jax.nn.sigmoid(og.astype(f32))).astype(p.dtype)


def retention(p):
    B, S, _ = p.shape
    f32 = jnp.float32
    H, Dh, C = RET_HEADS, RET_HEAD_DIM, RET_CHUNK
    q, k, v, g = _split(p, [RET_WIDTH] * 4)
    heads = lambda t: t.reshape(B, S, H, Dh).astype(f32)
    pos = jnp.arange(S, dtype=f32)
    q = rotary(heads(q), pos)
    k = rotary(heads(k), pos) * (Dh ** -0.5)
    v = heads(v)
    lg = jnp.log(1.0 - 2.0 ** (-5.0 - jnp.arange(H, dtype=f32)))
    n = S // C
    qc = q.reshape(B, n, C, H, Dh)
    kc = k.reshape(B, n, C, H, Dh)
    vc = v.reshape(B, n, C, H, Dh)
    idx = jnp.arange(C, dtype=f32)
    diff = idx[:, None] - idx[None, :]
    decay_in = jnp.where(diff >= 0, jnp.exp(jnp.maximum(diff, 0.0)[None] * lg[:, None, None]), 0.0)
    scores = jnp.einsum('bnihd,bnjhd->bnhij', qc, kc) * decay_in
    inner = jnp.einsum('bnhij,bnjhe->bnihe', scores, vc)
    q_decay = jnp.exp((idx[:, None] + 1.0) * lg[None, :])
    k_decay = jnp.exp((C - 1.0 - idx)[:, None] * lg[None, :])
    chunk_decay = jnp.exp(C * lg)
    chunk_kv = jnp.einsum('bnjhd,bnjhe->nbhde', kc * k_decay[:, :, None], vc)

    def step(state, kv):
        return state * chunk_decay[None, :, None, None] + kv, state

    _, prev = lax.scan(step, jnp.zeros((B, H, Dh, Dh), f32), chunk_kv)
    cross = jnp.einsum('bnihd,nbhde->bnihe', qc * q_decay[:, :, None], prev)
    o = (inner + cross).reshape(B, S, H, Dh)
    o = o * lax.rsqrt(jnp.mean(o * o, axis=-1, keepdims=True) + NORM_EPS)
    o = o.reshape(B, S, RET_WIDTH)
    return (jax.nn.silu(g.astype(f32)) * o).astype(p.dtype)


def token_shift(z, mu):
    z_prev = jnp.pad(z, ((0, 0), (1, 0), (0, 0)))[:, :-1]
    return z + (z_prev - z) * mu


def rwkv7_time_mix(p, mu, vec, w2, a2, g2, r_k, v_res):
    B, S, _ = p.shape
    f32 = jnp.float32
    H, N, W = RWKV_HEADS, RWKV_HEAD_DIM, RWKV_WIDTH
    z = token_shift(p.astype(f32), mu.astype(f32))
    r, k, v, wl, al, gl = _split(z, [W, W, W, RWKV_DECAY_LORA, RWKV_AAA_LORA, RWKV_GATE_LORA])
    vec = vec.astype(f32)
    w0, a0, k_k, k_a, lnx_w, lnx_b = vec[0], vec[1], vec[2], vec[3], vec[4], vec[5]
    log_decay = -jnp.exp(-jax.nn.softplus(-(w0 + jnp.tanh(wl) @ w2)) - 0.5)
    a = jax.nn.sigmoid(a0 + al @ a2)
    g = jax.nn.sigmoid(gl) @ g2
    v_layer = v
    if v_res is not None:
        p_vres, v0, v2, v_first = v_res
        v = v + (v_first - v) * jax.nn.sigmoid(v0 + p_vres.astype(f32) @ v2)
    heads = lambda t: t.reshape(B, S, H, N)
    kk = heads(k * k_k)
    kk = kk / jnp.maximum(jnp.sqrt(jnp.sum(kk * kk, axis=-1, keepdims=True)), 1e-12)
    k = k * (1.0 + (a - 1.0) * k_a)
    rh, kh, vh, ah = heads(r), heads(k), heads(v), heads(a)
    decay = jnp.exp(heads(log_decay))
    tm = lambda t: jnp.moveaxis(t, 1, 0)

    def step(state, inp):
        r_t, w_t, k_t, v_t, a_t, b_t = inp
        sa = jnp.einsum('bhvk,bhk->bhv', state, a_t)
        state = state * w_t[:, :, None, :] + sa[..., None] * b_t[:, :, None, :] + v_t[..., None] * k_t[:, :, None, :]
        return state, jnp.einsum('bhvk,bhk->bhv', state, r_t)

    _, y = lax.scan(step, jnp.zeros((B, H, N, N), f32),
                    (tm(rh), tm(decay), tm(kh), tm(vh), tm(-kk), tm(kk * ah)))
    y = jnp.moveaxis(y, 0, 1)
    mean = jnp.mean(y, axis=-1, keepdims=True)
    var = jnp.mean(jnp.square(y - mean), axis=-1, keepdims=True)
    y = ((y - mean) * lax.rsqrt(var + RWKV_LNX_EPS)).reshape(B, S, W) * lnx_w + lnx_b
    bonus = jnp.sum(rh * kh * r_k.astype(f32), axis=-1, keepdims=True) * vh
    y = (y + bonus.reshape(B, S, W)) * g
    return y.astype(p.dtype), v_layer


def setup_inputs(seed: int = 0) -> dict:
    key = jax.random.key(seed)
    ks = jax.random.split(key, 20)
    f32 = jnp.float32
    nrm = jax.random.normal
    W = RWKV_WIDTH
    x = nrm(ks[0], (BATCH, SEQ, D_MODEL), f32)
    norm_gains = 1.0 + 0.05 * nrm(ks[1], (DEPTH, 6, D_MODEL), f32)
    ffn_w_gu = nrm(ks[2], (DEPTH, 2, D_MODEL, 2 * D_FF), f32) * D_MODEL ** -0.5
    ffn_w_down = nrm(ks[3], (DEPTH, 2, D_FF, D_MODEL), f32) * D_FF ** -0.5
    w_in_first = nrm(ks[4], (D_MODEL, N_IN_FIRST), f32) * D_MODEL ** -0.5
    w_in_rest = nrm(ks[5], (DEPTH - 1, D_MODEL, N_IN_REST), f32) * D_MODEL ** -0.5
    w_out = nrm(ks[6], (DEPTH, D_MIX, D_MODEL), f32) * D_MIX ** -0.5
    fox_qk_gain = 1.0 + 0.05 * nrm(ks[7], (DEPTH, 2, FOX_HEAD_DIM), f32)
    fox_f_bias = 2.0 + 0.5 * nrm(ks[8], (DEPTH, FOX_HEADS), f32)
    rwkv_mu = jax.random.uniform(ks[9], (DEPTH, RWKV_SHIFT_COLS), f32)
    vec_off = jnp.array([0.0, 0.0, 0.85, 1.0, 1.0, 0.0], f32)
    vec_scale = jnp.array([0.5, 0.1, 0.05, 0.05, 0.05, 0.01], f32)
    rwkv_vec = vec_off[None, :, None] + vec_scale[None, :, None] * nrm(ks[10], (DEPTH, 6, W), f32)
    rwkv_w2 = 0.1 * nrm(ks[11], (DEPTH, RWKV_DECAY_LORA, W), f32) * RWKV_DECAY_LORA ** -0.5
    rwkv_a2 = 0.1 * nrm(ks[12], (DEPTH, RWKV_AAA_LORA, W), f32) * RWKV_AAA_LORA ** -0.5
    rwkv_g2 = nrm(ks[13], (DEPTH, RWKV_GATE_LORA, W), f32) * RWKV_GATE_LORA ** -0.5
    rwkv_r_k = 0.1 * nrm(ks[14], (DEPTH, RWKV_HEADS, RWKV_HEAD_DIM), f32)
    rwkv_v0 = 1.0 + 0.1 * nrm(ks[15], (DEPTH - 1, W), f32)
    rwkv_v2 = 0.1 * nrm(ks[16], (DEPTH - 1, RWKV_MV_LORA, W), f32) * RWKV_MV_LORA ** -0.5
    return {'x': x, 'norm_gains': norm_gains, 'ffn_w_gu': ffn_w_gu, 'ffn_w_down': ffn_w_down,
            'w_in_first': w_in_first, 'w_in_rest': w_in_rest, 'w_out': w_out,
            'fox_qk_gain': fox_qk_gain, 'fox_f_bias': fox_f_bias, 'rwkv_mu': rwkv_mu,
            'rwkv_vec': rwkv_vec, 'rwkv_w2': rwkv_w2, 'rwkv_a2': rwkv_a2, 'rwkv_g2': rwkv_g2,
            'rwkv_r_k': rwkv_r_k, 'rwkv_v0': rwkv_v0, 'rwkv_v2': rwkv_v2}


def reference(x, norm_gains, ffn_w_gu, ffn_w_down, w_in_first, w_in_rest, w_out,
              fox_qk_gain, fox_f_bias, rwkv_mu, rwkv_vec, rwkv_w2, rwkv_a2, rwkv_g2,
              rwkv_r_k, rwkv_v0, rwkv_v2):
    h = x
    v_first = None
    for l in range(DEPTH):
        g = norm_gains[l]
        h = h + 0.5 * rms_norm(swiglu(rms_norm(h, g[0]), ffn_w_gu[l, 0], ffn_w_down[l, 0]), g[1])
        u = rms_norm(h, g[2])
        if l == 0:
            p_fox, p_ret, p_rwkv = _split(u @ w_in_first, [FOX_COLS, RET_COLS, RWKV_SHIFT_COLS])
            v_res = None
        else:
            p_fox, p_ret, p_rwkv, p_vres = _split(u @ w_in_rest[l - 1],
                                                  [FOX_COLS, RET_COLS, RWKV_SHIFT_COLS, RWKV_MV_LORA])
            v_res = (p_vres, rwkv_v0[l - 1], rwkv_v2[l - 1], v_first)
        y_fox = forgetting_attention(p_fox, fox_qk_gain[l], fox_f_bias[l])
        y_ret = retention(p_ret)
        y_rwkv, v_layer = rwkv7_time_mix(p_rwkv, rwkv_mu[l], rwkv_vec[l], rwkv_w2[l], rwkv_a2[l],
                                         rwkv_g2[l], rwkv_r_k[l], v_res)
        if l == 0:
            v_first = v_layer
        y = jnp.concatenate([y_fox, y_ret, y_rwkv], axis=-1) @ w_out[l]
        h = h + rms_norm(y, g[3])
        h = h + 0.5 * rms_norm(swiglu(rms_norm(h, g[4]), ffn_w_gu[l, 1], ffn_w_down[l, 1]), g[5])
    return h
```

```python
import functools

import numpy as np
import jax
import jax.numpy as jnp
from jax import lax
from jax.experimental import pallas as pl
from jax.experimental.pallas import tpu as pltpu

F32 = jnp.float32
BF16 = jnp.bfloat16

FOX_H, FOX_D = 8, 128
FOX_W = FOX_H * FOX_D
RET_H, RET_D, RET_C = 4, 128, 128
RET_W = RET_H * RET_D
RW_H, RW_N = 8, 64
RW_W = RW_H * RW_N
RW_LORA_W, RW_LORA_A, RW_LORA_G, RW_LORA_V = 96, 96, 256, 64
ROPE_BASE = 10000.0
NORM_EPS = 1e-6
RW_LNX_EPS = 64e-5

LANES = 128
V7X_VMEM_BYTES = 64 * 1024 * 1024
VMEM_HEADROOM_BYTES = 6 * 1024 * 1024

RW_C = 64
RW_G = 4
RW_L = RW_G * RW_N
RW_GC = RW_G * RW_C
RW_NG = RW_H // RW_G

C_FOX_Q, C_FOX_K, C_FOX_V, C_FOX_O = 0, 1024, 2048, 3072
C_RET = 4096
C_RW = 6144
C_LORA = 7680
LORA_BLOCK = 640
C_F = C_LORA + LORA_BLOCK
N_PROJ = C_F + LANES


def _vmem_limit(block_bytes):
    want = int(block_bytes * 1.25) + 12 * 1024 * 1024
    return min(want, V7X_VMEM_BYTES - VMEM_HEADROOM_BYTES)


def _dot(a, b):
    return jnp.dot(a, b, preferred_element_type=F32)


def _dot_nt(a, b):
    return lax.dot_general(a, b, (((1,), (1,)), ((), ())), preferred_element_type=F32)


def _dot_tn(a, b):
    return lax.dot_general(a, b, (((0,), (0,)), ((), ())), preferred_element_type=F32)


def _split2(x):
    hi = x.astype(BF16)
    lo = (x - hi.astype(F32)).astype(BF16)
    return hi, lo


def _split3(x):
    hi = x.astype(BF16)
    r1 = x - hi.astype(F32)
    mid = r1.astype(BF16)
    lo = (r1 - mid.astype(F32)).astype(BF16)
    return hi, mid, lo


def _dot_hi(a, b):
    ah, al = _split2(a)
    bh, bl = _split2(b)
    return _dot(ah, bh) + _dot(al, bh) + _dot(ah, bl)


def _dot_exact_rhs(a, b_bf16):
    ah, al = _split2(a)
    return _dot(ah, b_bf16) + _dot(al, b_bf16)


def _rms(x, g, eps=NORM_EPS):
    ms = jnp.mean(x * x, axis=-1, keepdims=True)
    return x * lax.rsqrt(ms + eps) * g


def _sigmoid(x):
    return 1.0 / (1.0 + jnp.exp(-x))


def _softplus(x):
    return jnp.maximum(x, 0.0) + jnp.log(1.0 + jnp.exp(-jnp.abs(x)))


def _ffn_body(h_ref, gi_ref, go_ref, wg_ref, wu_ref, wd_ref, o_ref, xn_ref, acc_ref):
    k = pl.program_id(1)

    @pl.when(k == 0)
    def _():
        xn_ref[...] = _rms(h_ref[...], gi_ref[...]).astype(BF16)
        acc_ref[...] = jnp.zeros_like(acc_ref)

    xn = xn_ref[...]
    gate = _dot(xn, wg_ref[...])
    up = _dot(xn, wu_ref[...])
    act = (gate * _sigmoid(gate) * up).astype(BF16)
    acc_ref[...] += _dot(act, wd_ref[...])

    @pl.when(k == pl.num_programs(1) - 1)
    def _():
        o_ref[...] = h_ref[...] + 0.5 * _rms(acc_ref[...], go_ref[...])


def _ffn(h, g_in, g_out, w_gu, w_down, *, tm, tf):
    t, d = h.shape
    f = w_down.shape[0]
    nk = f // tf
    blocks = 2 * (tm * d * 4) * 2 + 2 * 3 * (d * tf * 2) + tm * d * 6
    return pl.pallas_call(
        _ffn_body,
        grid=(t // tm, nk),
        in_specs=[
            pl.BlockSpec((tm, d), lambda i, k: (i, 0)),
            pl.BlockSpec((1, d), lambda i, k: (0, 0)),
            pl.BlockSpec((1, d), lambda i, k: (0, 0)),
            pl.BlockSpec((d, tf), lambda i, k: (0, k)),
            pl.BlockSpec((d, tf), lambda i, k: (0, k + nk)),
            pl.BlockSpec((tf, d), lambda i, k: (k, 0)),
        ],
        out_specs=pl.BlockSpec((tm, d), lambda i, k: (i, 0)),
        out_shape=jax.ShapeDtypeStruct((t, d), F32),
        scratch_shapes=[pltpu.VMEM((tm, d), BF16), pltpu.VMEM((tm, d), F32)],
        compiler_params=pltpu.CompilerParams(
            dimension_semantics=("parallel", "arbitrary"),
            vmem_limit_bytes=_vmem_limit(blocks)),
        name="ffn",
    )(h, g_in, g_out, w_gu, w_gu, w_down)


def _inproj_body(h_ref, g_ref, w_ref, o_ref, xn_ref):
    @pl.when(pl.program_id(1) == 0)
    def _():
        xn_ref[...] = _rms(h_ref[...], g_ref[...]).astype(BF16)

    o_ref[...] = _dot(xn_ref[...], w_ref[...])


def _inproj(h, g, w_all, *, tm, tn):
    t, d = h.shape
    n = w_all.shape[1]
    blocks = 2 * (tm * d * 4) + 2 * (d * tn * 2) + 2 * (tm * tn * 4) + tm * d * 2
    return pl.pallas_call(
        _inproj_body,
        grid=(t // tm, n // tn),
        in_specs=[
            pl.BlockSpec((tm, d), lambda i, j: (i, 0)),
            pl.BlockSpec((1, d), lambda i, j: (0, 0)),
            pl.BlockSpec((d, tn), lambda i, j: (0, j)),
        ],
        out_specs=pl.BlockSpec((tm, tn), lambda i, j: (i, j)),
        out_shape=jax.ShapeDtypeStruct((t, n), F32),
        scratch_shapes=[pltpu.VMEM((tm, d), BF16)],
        compiler_params=pltpu.CompilerParams(
            dimension_semantics=("parallel", "arbitrary"),
            vmem_limit_bytes=_vmem_limit(blocks)),
        name="inproj",
    )(h, g, w_all)


def _foxprep_body(q_ref, k_ref, v_ref, f_ref, gq_ref, gk_ref, fb_ref, ltri_ref, selq_ref,
                  selk_ref, oneq_ref, onek_ref, qt_ref, ka_ref, vt_ref, carry_ref):
    @pl.when(pl.program_id(1) == 0)
    def _():
        carry_ref[...] = jnp.zeros_like(carry_ref)

    tm = f_ref.shape[1]
    x = f_ref[0] + fb_ref[...]
    logf = -_softplus(-x)
    l3 = jnp.concatenate(_split3(logf), axis=1)
    c3 = _dot(ltri_ref[...], l3)
    cum = c3[:, :LANES] + c3[:, LANES:2 * LANES] + c3[:, 2 * LANES:] + carry_ref[...]
    carry_ref[...] = cum[tm - 1:tm, :]
    cc = jnp.concatenate(_split3(cum), axis=1)

    q = q_ref[0]
    k = k_ref[0]
    v = v_ref[0]
    for h in range(FOX_H):
        sl = slice(h * FOX_D, (h + 1) * FOX_D)
        qn = _rms(q[:, sl], gq_ref[...]) * (FOX_D ** -0.5)
        kn = _rms(k[:, sl], gk_ref[...])
        eq = _dot(cc, selq_ref[h]) + oneq_ref[...]
        ek = _dot(cc, selk_ref[h]) + onek_ref[...]
        qt_ref[0, h] = jnp.concatenate([qn, eq], axis=1).T.astype(BF16)
        ka_ref[0, h] = jnp.concatenate([kn, ek], axis=1).astype(BF16)
        vt_ref[0, h] = v[:, sl].T.astype(BF16)


def _fox_tables(tm):
    ltri = np.tril(np.ones((tm, tm), np.float32))
    selq = np.zeros((FOX_H, 3 * LANES, LANES), np.float32)
    selk = np.zeros((FOX_H, 3 * LANES, LANES), np.float32)
    for h in range(FOX_H):
        for j in range(3):
            selq[h, j * LANES + h, j] = 1.0
            selk[h, j * LANES + h, 3 + j] = -1.0
    oneq = np.zeros((1, LANES), np.float32)
    onek = np.zeros((1, LANES), np.float32)
    oneq[0, 3:6] = 1.0
    onek[0, 0:3] = 1.0
    return (jnp.asarray(ltri, BF16), jnp.asarray(selq, BF16), jnp.asarray(selk, BF16),
            jnp.asarray(oneq), jnp.asarray(onek))


def _foxprep(p3, gq, gk, fbias, *, tm):
    b, s, _ = p3.shape
    ltri, selq, selk, oneq, onek = _fox_tables(tm)
    da = 2 * FOX_D
    const = lambda *shape: pl.BlockSpec(shape, lambda bi, i: (0,) * len(shape))
    blocks = 2 * (3 * tm * FOX_W * 4 + tm * LANES * 4) + 2 * (tm * FOX_H * (2 * da + FOX_D) * 2) \
        + 2 * (tm * tm * 2 + 2 * FOX_H * 3 * LANES * LANES * 2)
    return pl.pallas_call(
        _foxprep_body,
        grid=(b, s // tm),
        in_specs=[
            pl.BlockSpec((1, tm, FOX_W), lambda bi, i: (bi, i, C_FOX_Q // FOX_W)),
            pl.BlockSpec((1, tm, FOX_W), lambda bi, i: (bi, i, C_FOX_K // FOX_W)),
            pl.BlockSpec((1, tm, FOX_W), lambda bi, i: (bi, i, C_FOX_V // FOX_W)),
            pl.BlockSpec((1, tm, LANES), lambda bi, i: (bi, i, C_F // LANES)),
            const(1, FOX_D), const(1, FOX_D), const(1, LANES), const(tm, tm),
            const(FOX_H, 3 * LANES, LANES), const(FOX_H, 3 * LANES, LANES),
            const(1, LANES), const(1, LANES),
        ],
        out_specs=[
            pl.BlockSpec((1, FOX_H, da, tm), lambda bi, i: (bi, 0, 0, i)),
            pl.BlockSpec((1, FOX_H, tm, da), lambda bi, i: (bi, 0, i, 0)),
            pl.BlockSpec((1, FOX_H, FOX_D, tm), lambda bi, i: (bi, 0, 0, i)),
        ],
        out_shape=[
            jax.ShapeDtypeStruct((b, FOX_H, da, s), BF16),
            jax.ShapeDtypeStruct((b, FOX_H, s, da), BF16),
            jax.ShapeDtypeStruct((b, FOX_H, FOX_D, s), BF16),
        ],
        scratch_shapes=[pltpu.VMEM((1, LANES), F32)],
        compiler_params=pltpu.CompilerParams(
            dimension_semantics=("parallel", "arbitrary"),
            vmem_limit_bytes=_vmem_limit(blocks)),
        name="foxprep",
    )(p3, p3, p3, p3, gq, gk, fbias, ltri, selq, selk, oneq, onek)


def _fox_body(qt_ref, ka_ref, vt_ref, og_ref, o_ref, m_ref, l_ref, acc_ref, *, tb):
    i = pl.program_id(2)
    qt = qt_ref[0, 0]
    m_ref[...] = jnp.full_like(m_ref, -1e30)
    l_ref[...] = jnp.zeros_like(l_ref)
    acc_ref[...] = jnp.zeros_like(acc_ref)

    def step(j, masked):
        off = pl.multiple_of(j * tb, tb)
        s = _dot(ka_ref[0, 0, pl.ds(off, tb), :], qt)
        if masked:
            kk = lax.broadcasted_iota(jnp.int32, (tb, tb), 0)
            qq = lax.broadcasted_iota(jnp.int32, (tb, tb), 1)
            s = jnp.where(kk <= qq, s, -1e30)
        m_prev = m_ref[...]
        m_new = jnp.maximum(m_prev, jnp.max(s, axis=0, keepdims=True))
        alpha = jnp.exp(m_prev - m_new)
        p = jnp.exp(s - m_new)
        l_ref[...] = alpha * l_ref[...] + jnp.sum(p, axis=0, keepdims=True)
        acc_ref[...] = alpha * acc_ref[...] + _dot(vt_ref[0, 0, :, pl.ds(off, tb)], p.astype(BF16))
        m_ref[...] = m_new

    def body(j, carry):
        step(j, False)
        return carry

    lax.fori_loop(0, i, body, 0)
    step(i, True)
    o = (acc_ref[...] / l_ref[...]).T
    o_ref[0] = (o * _sigmoid(og_ref[0])).astype(o_ref.dtype)


def _fox_attention(qt, ka, vt, p3, *, tb):
    b, h, da, s = qt.shape
    blocks = 2 * (da * tb * 2 + s * da * 2 + FOX_D * s * 2 + tb * LANES * 4 + tb * LANES * 2) \
        + (FOX_D + 16) * tb * 4
    return pl.pallas_call(
        functools.partial(_fox_body, tb=tb),
        grid=(b, h, s // tb),
        in_specs=[
            pl.BlockSpec((1, 1, da, tb), lambda bi, hi, i: (bi, hi, 0, i)),
            pl.BlockSpec((1, 1, s, da), lambda bi, hi, i: (bi, hi, 0, 0)),
            pl.BlockSpec((1, 1, FOX_D, s), lambda bi, hi, i: (bi, hi, 0, 0)),
            pl.BlockSpec((1, tb, FOX_D), lambda bi, hi, i: (bi, i, C_FOX_O // FOX_D + hi)),
        ],
        out_specs=pl.BlockSpec((1, tb, FOX_D), lambda bi, hi, i: (bi, i, hi)),
        out_shape=jax.ShapeDtypeStruct((b, s, FOX_W), BF16),
        scratch_shapes=[pltpu.VMEM((1, tb), F32), pltpu.VMEM((1, tb), F32),
                        pltpu.VMEM((FOX_D, tb), F32)],
        compiler_params=pltpu.CompilerParams(
            dimension_semantics=("parallel", "parallel", "arbitrary"),
            vmem_limit_bytes=_vmem_limit(blocks)),
        name="fox_attention",
    )(qt, ka, vt, p3)


def _ret_body(q_ref, k_ref, v_ref, g_ref, cos_ref, sin_ref, din_ref, qd_ref, kd_ref, cd_ref,
              o_ref, st_ref):
    @pl.when(pl.program_id(1) == 0)
    def _():
        st_ref[...] = jnp.zeros_like(st_ref)

    cos = cos_ref[...]
    sin = sin_ref[...]
    half = RET_D // 2
    outs = []
    for h in range(RET_H):
        sl = slice(h * RET_D, (h + 1) * RET_D)
        q = q_ref[0][:, sl]
        k = k_ref[0][:, sl]
        v = v_ref[0][:, sl].astype(BF16)
        g = g_ref[0][:, sl]
        qr = q * cos + pltpu.roll(q, half, 1) * sin
        kr = (k * cos + pltpu.roll(k, half, 1) * sin) * (RET_D ** -0.5)
        scores = _dot_nt(qr.astype(BF16), kr.astype(BF16)) * din_ref[h]
        st = st_ref[h]
        o = _dot(scores.astype(BF16), v) + _dot((qr * qd_ref[h]).astype(BF16), st.astype(BF16))
        st_ref[h] = st * cd_ref[h] + _dot_tn((kr * kd_ref[h]).astype(BF16), v)
        o = o * lax.rsqrt(jnp.mean(o * o, axis=-1, keepdims=True) + NORM_EPS)
        outs.append(g * _sigmoid(g) * o)
    o_ref[0] = jnp.concatenate(outs, axis=1).astype(o_ref.dtype)


def _ret_tables(s):
    c = RET_C
    lg = np.log(1.0 - 2.0 ** (-5.0 - np.arange(RET_H, dtype=np.float64)))
    idx = np.arange(c, dtype=np.float64)
    diff = idx[:, None] - idx[None, :]
    din = np.where(diff >= 0, np.exp(np.maximum(diff, 0.0)[None] * lg[:, None, None]), 0.0)
    qd = np.exp((idx[None, :, None] + 1.0) * lg[:, None, None]) * np.ones((1, 1, RET_D))
    kd = np.exp((c - 1.0 - idx)[None, :, None] * lg[:, None, None]) * np.ones((1, 1, RET_D))
    cd = np.exp(c * lg)[:, None, None] * np.ones((1, 1, RET_D))
    half = RET_D // 2
    pos = jnp.arange(s, dtype=F32)
    inv = ROPE_BASE ** (-jnp.arange(half, dtype=F32) / half)
    ang = pos[:, None] * inv[None, :]
    cos = jnp.cos(ang)
    sin = jnp.sin(ang)
    cos2 = jnp.concatenate([cos, cos], axis=1)
    sin2 = jnp.concatenate([-sin, sin], axis=1)
    f = lambda a: jnp.asarray(a, F32)
    return cos2, sin2, f(din), f(qd), f(kd), f(cd)


def _retention(p3):
    b, s, _ = p3.shape
    c = RET_C
    cos2, sin2, din, qd, kd, cd = _ret_tables(s)
    col = lambda j: pl.BlockSpec((1, c, RET_W), lambda bi, i: (bi, i, C_RET // RET_W + j))
    const3 = lambda n: pl.BlockSpec((RET_H, n, RET_D), lambda bi, i: (0, 0, 0))
    blocks = 2 * (4 * c * RET_W * 4 + 2 * c * RET_D * 4 + c * RET_W * 2) \
        + 2 * RET_H * (3 * c + 1) * RET_D * 4 + RET_H * RET_D * RET_D * 4
    return pl.pallas_call(
        _ret_body,
        grid=(b, s // c),
        in_specs=[col(0), col(1), col(2), col(3),
                  pl.BlockSpec((c, RET_D), lambda bi, i: (i, 0)),
                  pl.BlockSpec((c, RET_D), lambda bi, i: (i, 0)),
                  const3(c), const3(c), const3(c), const3(1)],
        out_specs=pl.BlockSpec((1, c, RET_W), lambda bi, i: (bi, i, 0)),
        out_shape=jax.ShapeDtypeStruct((b, s, RET_W), BF16),
        scratch_shapes=[pltpu.VMEM((RET_H, RET_D, RET_D), F32)],
        compiler_params=pltpu.CompilerParams(
            dimension_semantics=("parallel", "arbitrary"),
            vmem_limit_bytes=_vmem_limit(blocks)),
        name="retention",
    )(p3, p3, p3, p3, cos2, sin2, din, qd, kd, cd)


def _rwprep_body(*refs, tm, has_vres):
    if has_vres:
        (main_ref, lora_ref, pmain_ref, plora_ref, mum_ref, mul_ref, vec_ref, w2_ref, a2_ref,
         g2_ref, bd_ref, v2_ref, v0_ref, vf_ref,
         r_o, k_o, v_o, lw_o, a_o, b_o, g_o) = refs
    else:
        (main_ref, lora_ref, pmain_ref, plora_ref, mum_ref, mul_ref, vec_ref, w2_ref, a2_ref,
         g2_ref, bd_ref,
         r_o, k_o, v_o, lw_o, a_o, b_o, g_o, vs_o) = refs
    first = pl.program_id(1) == 0

    def shift(x, prev8, mu):
        prev_row = jnp.where(first, 0.0, prev8[7:8, :])
        xp = pltpu.roll(x, 1, 0)
        rows = lax.broadcasted_iota(jnp.int32, x.shape, 0)
        xp = jnp.where(rows == 0, prev_row, xp)
        return x + (xp - x) * mu

    nl = 2 * LANES + RW_LORA_G
    zm = shift(main_ref[0], pmain_ref[0], mum_ref[...])
    zl = shift(lora_ref[0][:, :nl], plora_ref[0][:, :nl], mul_ref[...])
    r = zm[:, :RW_W]
    k = zm[:, RW_W:2 * RW_W]
    v = zm[:, 2 * RW_W:]
    wl = zl[:, :LANES]
    al = zl[:, LANES:2 * LANES]
    gl = zl[:, 2 * LANES:]
    vec = vec_ref[...]
    w0, a0, k_k, k_a = vec[0:1], vec[1:2], vec[2:3], vec[3:4]

    zw = w0 + _dot_hi(jnp.tanh(wl), w2_ref[...])
    logw = -jnp.exp(-_softplus(-zw) - 0.5)
    a = _sigmoid(a0 + _dot_hi(al, a2_ref[...]))
    g = _dot_hi(_sigmoid(gl), g2_ref[...])
    if has_vres:
        pv = lora_ref[0][:, nl:]
        v_eff = v + (vf_ref[0] - v) * _sigmoid(v0_ref[...] + _dot_hi(pv, v2_ref[...]))
    else:
        v_eff = v
        vs_o[0] = v
    kk = k * k_k
    ss = _dot_exact_rhs(kk * kk, bd_ref[...])
    kkn = kk / jnp.maximum(jnp.sqrt(ss), 1e-12)
    r_o[0] = r
    k_o[0] = k * (1.0 + (a - 1.0) * k_a)
    v_o[0] = v_eff
    lw_o[0] = logw
    a_o[0] = -kkn
    b_o[0] = kkn * a
    g_o[0] = g


def _head_blockdiag(width, n):
    idx = np.arange(width) // n
    return jnp.asarray((idx[:, None] == idx[None, :]).astype(np.float32), BF16)


def _rwprep(p3, mu, vec, w2, a2, g2, vres, *, tm):
    b, s, _ = p3.shape
    has_vres = vres is not None
    nl = 2 * LANES + RW_LORA_G
    pad_rows = lambda w, n: jnp.pad(w, ((0, n - w.shape[0]), (0, 0)))
    pad_cols = lambda m, n: jnp.pad(m, (0, n - m.shape[0]))
    w_main = 3 * RW_W
    mu_main = mu[:w_main][None, :]
    o1 = w_main + RW_LORA_W
    o2 = o1 + RW_LORA_A
    mu_lora = jnp.concatenate([pad_cols(mu[w_main:o1], LANES), pad_cols(mu[o1:o2], LANES),
                               mu[o2:]])[None, :]
    vec8 = pad_rows(vec, 8)
    bd = _head_blockdiag(RW_W, RW_N)
    rows8 = tm // 8
    prev = lambda bi, i: jnp.maximum(i * rows8 - 1, 0)
    const = lambda *shape: pl.BlockSpec(shape, lambda bi, i: (0,) * len(shape))
    in_specs = [
        pl.BlockSpec((1, tm, w_main), lambda bi, i: (bi, i, C_RW // w_main)),
        pl.BlockSpec((1, tm, LORA_BLOCK), lambda bi, i: (bi, i, C_LORA // LORA_BLOCK)),
        pl.BlockSpec((1, 8, w_main), lambda bi, i: (bi, prev(bi, i), C_RW // w_main)),
        pl.BlockSpec((1, 8, LORA_BLOCK), lambda bi, i: (bi, prev(bi, i), C_LORA // LORA_BLOCK)),
        const(1, w_main), const(1, nl), const(8, RW_W), const(LANES, RW_W), const(LANES, RW_W),
        const(RW_LORA_G, RW_W), const(RW_W, RW_W),
    ]
    args = [p3, p3, p3, p3, mu_main, mu_lora, vec8, pad_rows(w2, LANES), pad_rows(a2, LANES), g2, bd]
    tok = pl.BlockSpec((1, tm, RW_W), lambda bi, i: (bi, i, 0))
    n_out = 7
    if has_vres:
        v0, v2, v_first = vres
        in_specs += [const(LANES, RW_W), const(1, RW_W), tok]
        args += [pad_rows(v2, LANES), v0[None, :], v_first]
    else:
        n_out += 1
    blocks = 2 * (tm + 8) * (w_main + LORA_BLOCK) * 4 + 2 * (n_out + 1) * tm * RW_W * 4 \
        + 2 * (4 * LANES + RW_LORA_G + RW_W) * RW_W * 4
    return pl.pallas_call(
        functools.partial(_rwprep_body, tm=tm, has_vres=has_vres),
        grid=(b, s // tm),
        in_specs=in_specs,
        out_specs=[tok] * n_out,
        out_shape=[jax.ShapeDtypeStruct((b, s, RW_W), F32)] * n_out,
        compiler_params=pltpu.CompilerParams(
            dimension_semantics=("parallel", "parallel"),
            vmem_limit_bytes=_vmem_limit(blocks)),
        name="rwkv_prep",
    )(*args)


def _rwchunk_body(r_ref, k_ref, v_ref, lw_ref, a_ref, b_ref, g_ref, vec_ref, rk_ref, bd_ref,
                  ltri_ref, y_ref, st_ref):
    @pl.when(pl.program_id(1) == 0)
    def _():
        st_ref[...] = jnp.zeros_like(st_ref)

    c, l, gc = RW_C, RW_L, RW_GC
    row = lax.broadcasted_iota(jnp.int32, (gc, l), 0)
    lane = lax.broadcasted_iota(jnp.int32, (gc, l), 1)
    own = (row // c) == (lane // RW_N)
    ii = lax.broadcasted_iota(jnp.int32, (gc, gc), 0)
    jj = lax.broadcasted_iota(jnp.int32, (gc, gc), 1)
    strict = ii > jj
    incl = ii >= jj
    eye = ii == jj

    def stack(x):
        return jnp.where(own, jnp.concatenate([x] * RW_G, axis=0), 0.0).astype(BF16)

    ltri = ltri_ref[...]
    ys = []
    for gi in range(RW_NG):
        sl = slice(gi * l, (gi + 1) * l)
        lw = lw_ref[0][:, sl]
        hi, mid, lo = _split3(lw)
        cum = _dot(ltri, hi) + _dot(ltri, mid) + _dot(ltri, lo)
        pin = jnp.exp(cum)
        pex = jnp.exp(cum - lw)
        pinv = jnp.exp(-cum)
        pc = pin[c - 1:c, :]
        bt = b_ref[0][:, sl] * pinv
        kt = k_ref[0][:, sl] * pinv
        a_s = stack(a_ref[0][:, sl] * pex)
        b_s = stack(bt)
        k_s = stack(kt)
        r_s = stack(r_ref[0][:, sl] * pin)
        v_s = stack(v_ref[0][:, sl])
        bh_s = stack(bt * pc)
        kh_s = stack(kt * pc)

        a_ab = jnp.where(strict, _dot_nt(a_s, b_s), 0.0)
        a_ak = jnp.where(strict, _dot_nt(a_s, k_s), 0.0)
        a_rb = jnp.where(incl, _dot_nt(r_s, b_s), 0.0)
        a_rk = jnp.where(incl, _dot_nt(r_s, k_s), 0.0)

        pw = a_ab
        tinv = jnp.where(eye, 1.0, 0.0) + a_ab
        for _ in range(int(np.log2(c)) - 1):
            pwb = pw.astype(BF16)
            pw = _dot(pwb, pwb)
            tinv = tinv + _dot(pw.astype(BF16), tinv.astype(BF16))

        akv = _dot(a_ak.astype(BF16), v_s)
        x = _dot(tinv.astype(BF16), jnp.concatenate([a_s, akv.astype(BF16)], axis=1))
        xb = x.astype(BF16)
        ry = _dot(a_rb.astype(BF16), xb)
        rbar = r_s.astype(F32) + ry[:, :l]
        yi = ry[:, l:] + _dot(a_rk.astype(BF16), v_s)
        mt = jnp.where(eye, pc, 0.0) + _dot_tn(bh_s, xb[:, :l])
        nt = _dot_tn(bh_s, xb[:, l:]) + _dot_tn(kh_s, v_s)

        s0 = st_ref[gi].astype(BF16)
        ysk = _dot(rbar.astype(BF16), s0) + yi
        st_ref[gi] = _dot(mt.astype(BF16), s0) + nt
        y = ysk[0:c]
        for h in range(1, RW_G):
            y = y + ysk[h * c:(h + 1) * c]
        ys.append(y)

    y = jnp.concatenate(ys, axis=1)
    bd = bd_ref[...]
    vec = vec_ref[...]
    lnx_w, lnx_b = vec[4:5], vec[5:6]
    mean = _dot_exact_rhs(y, bd) * (1.0 / RW_N)
    d = y - mean
    var = _dot_exact_rhs(d * d, bd) * (1.0 / RW_N)
    yn = d * lax.rsqrt(var + RW_LNX_EPS) * lnx_w + lnx_b
    rr = r_ref[0]
    kk = k_ref[0]
    vv = v_ref[0]
    bonus = _dot_exact_rhs(rr * kk * rk_ref[...], bd) * vv
    y_ref[0] = ((yn + bonus) * g_ref[0]).astype(y_ref.dtype)


def _rwchunk(r, k, v, lw, a, b, g, vec, r_k):
    bsz, s, _ = r.shape
    c = RW_C
    bd = _head_blockdiag(RW_W, RW_N)
    ltri = jnp.asarray(np.tril(np.ones((c, c), np.float32)), BF16)
    tok = pl.BlockSpec((1, c, RW_W), lambda bi, i: (bi, i, 0))
    const = lambda *shape: pl.BlockSpec(shape, lambda bi, i: (0,) * len(shape))
    blocks = 2 * (7 * c * RW_W * 4 + c * RW_W * 2) + 2 * (RW_W * RW_W * 2 + 9 * RW_W * 4) \
        + RW_NG * RW_L * RW_L * 4
    return pl.pallas_call(
        _rwchunk_body,
        grid=(bsz, s // c),
        in_specs=[tok] * 7 + [const(8, RW_W), const(1, RW_W), const(RW_W, RW_W), const(c, c)],
        out_specs=tok,
        out_shape=jax.ShapeDtypeStruct((bsz, s, RW_W), BF16),
        scratch_shapes=[pltpu.VMEM((RW_NG, RW_L, RW_L), F32)],
        compiler_params=pltpu.CompilerParams(
            dimension_semantics=("parallel", "arbitrary"),
            vmem_limit_bytes=_vmem_limit(blocks)),
        name="rwkv_chunk",
    )(r, k, v, lw, a, b, g, jnp.pad(vec, ((0, 8 - vec.shape[0]), (0, 0))), r_k.reshape(1, RW_W),
      bd, ltri)


def _outproj_body(yf_ref, yr_ref, yw_ref, wf_ref, wr_ref, ww_ref, h_ref, g_ref, o_ref):
    y = _dot(yf_ref[...], wf_ref[...]) + _dot(yr_ref[...], wr_ref[...]) + _dot(yw_ref[...], ww_ref[...])
    o_ref[...] = h_ref[...] + _rms(y, g_ref[...])


def _outproj(yf, yr, yw, w_out, h, g, *, tm):
    t, d = h.shape
    wf = w_out[:FOX_W]
    wr = w_out[FOX_W:FOX_W + RET_W]
    ww = w_out[FOX_W + RET_W:]
    row = lambda w: pl.BlockSpec((tm, w), lambda i: (i, 0))
    whole = lambda w: pl.BlockSpec((w, d), lambda i: (0, 0))
    blocks = 2 * tm * (FOX_W + RET_W + RW_W) * 2 + 2 * (FOX_W + RET_W + RW_W) * d * 2 + 4 * tm * d * 4
    return pl.pallas_call(
        _outproj_body,
        grid=(t // tm,),
        in_specs=[row(FOX_W), row(RET_W), row(RW_W), whole(FOX_W), whole(RET_W), whole(RW_W),
                  row(d), pl.BlockSpec((1, d), lambda i: (0, 0))],
        out_specs=row(d),
        out_shape=jax.ShapeDtypeStruct((t, d), F32),
        compiler_params=pltpu.CompilerParams(
            dimension_semantics=("parallel",),
            vmem_limit_bytes=_vmem_limit(blocks)),
        name="outproj",
    )(yf, yr, yw, wf, wr, ww, h, g)


def _arrange_w_in(w_in):
    d = w_in.shape[0]
    o_f = 4 * FOX_W
    o_ret = o_f + FOX_H
    o_rw = o_ret + 4 * RET_W
    o_wl = o_rw + 3 * RW_W
    o_al = o_wl + RW_LORA_W
    o_gl = o_al + RW_LORA_A
    o_vr = o_gl + RW_LORA_G
    padc = lambda m, n: jnp.pad(m, ((0, 0), (0, n - m.shape[1])))
    vres = w_in[:, o_vr:o_vr + RW_LORA_V] if w_in.shape[1] > o_vr else jnp.zeros((d, 0), w_in.dtype)
    cols = [w_in[:, :o_f], w_in[:, o_ret:o_rw], w_in[:, o_rw:o_wl],
            padc(w_in[:, o_wl:o_al], LANES), padc(w_in[:, o_al:o_gl], LANES), w_in[:, o_gl:o_vr],
            padc(vres, LANES), padc(w_in[:, o_f:o_ret], LANES)]
    w_all = jnp.concatenate(cols, axis=1).astype(BF16)
    assert w_all.shape[1] == N_PROJ
    return w_all


def _tiles(seq, d_ff):
    pick = lambda n, cands: next(c for c in cands if n % c == 0)
    return dict(
        ffn_tm=pick(seq, (512, 256, 128)), ffn_tf=pick(d_ff, (512, 256, 128)),
        proj_tm=pick(seq, (512, 256, 128)), proj_tn=pick(N_PROJ, (768, 384, 128)),
        foxprep_tm=pick(seq, (512, 256, 128)), fox_tb=pick(seq, (512, 256, 128)),
        rwprep_tm=pick(seq, (256, 128)), out_tm=pick(seq, (512, 256, 128)),
    )


def kernel(x, norm_gains, ffn_w_gu, ffn_w_down, w_in_first, w_in_rest, w_out, fox_qk_gain, fox_f_bias,
           rwkv_mu, rwkv_vec, rwkv_w2, rwkv_a2, rwkv_g2, rwkv_r_k, rwkv_v0, rwkv_v2):
    bsz, seq, d = x.shape
    depth = norm_gains.shape[0]
    d_ff = ffn_w_down.shape[2]
    assert seq % RET_C == 0 and seq % RW_C == 0 and d == FOX_W + RET_W + RW_W
    tl = _tiles(seq, d_ff)
    t = bsz * seq
    h = x.reshape(t, d)
    w_gu = ffn_w_gu.astype(BF16)
    w_down = ffn_w_down.astype(BF16)
    w_o = w_out.astype(BF16)
    v_first = None
    for l in range(depth):
        g = norm_gains[l]
        h = _ffn(h, g[0:1], g[1:2], w_gu[l, 0], w_down[l, 0], tm=tl["ffn_tm"], tf=tl["ffn_tf"])

        w_in = w_in_first if l == 0 else w_in_rest[l - 1]
        p = _inproj(h, g[2:3], _arrange_w_in(w_in), tm=tl["proj_tm"], tn=tl["proj_tn"])
        p3 = p.reshape(bsz, seq, N_PROJ)

        fbias = jnp.pad(fox_f_bias[l], (0, LANES - FOX_H))[None, :]
        qt, ka, vt = _foxprep(p3, fox_qk_gain[l, 0:1], fox_qk_gain[l, 1:2], fbias, tm=tl["foxprep_tm"])
        y_fox = _fox_attention(qt, ka, vt, p3, tb=tl["fox_tb"])

        y_ret = _retention(p3)

        vres = None if l == 0 else (rwkv_v0[l - 1], rwkv_v2[l - 1], v_first)
        outs = _rwprep(p3, rwkv_mu[l], rwkv_vec[l], rwkv_w2[l], rwkv_a2[l], rwkv_g2[l], vres,
                       tm=tl["rwprep_tm"])
        if l == 0:
            v_first = outs[7]
        y_rw = _rwchunk(*outs[:7], rwkv_vec[l], rwkv_r_k[l])

        h = _outproj(y_fox.reshape(t, FOX_W), y_ret.reshape(t, RET_W), y_rw.reshape(t, RW_W),
                     w_o[l], h, g[3:4], tm=tl["out_tm"])
        h = _ffn(h, g[4:5], g[5:6], w_gu[l, 1], w_down[l, 1], tm=tl["ffn_tm"], tf=tl["ffn_tf"])
    return h.reshape(bsz, seq, d)
```
